```python
import jax, jax.numpy as jnp
from jax import lax
import numpy as np

D_MODEL = 1024
BATCH = 4
SEQ = 8192
DEPTH = 2

N_EVEN = (DEPTH + 1) // 2
N_ODD = DEPTH // 2

POOL_WINDOWS = (2, 4, 8, 16)
POOL_GROUPS = len(POOL_WINDOWS)
POOL_GROUP_DIM = D_MODEL // POOL_GROUPS

SSM_EXPAND = 2
SSM_D_INNER = SSM_EXPAND * D_MODEL
SSM_HEAD_DIM = 64
SSM_N_HEADS = SSM_D_INNER // SSM_HEAD_DIM
SSM_N_GROUPS = 8
SSM_HEADS_PER_GROUP = SSM_N_HEADS // SSM_N_GROUPS
SSM_D_STATE = 128
SSM_CONV = 4
SSM_CHUNK = 128
SSM_BC_DIM = SSM_N_GROUPS * SSM_D_STATE
SSM_CONV_DIM = SSM_D_INNER + 2 * SSM_BC_DIM
SSM_IN_DIM = SSM_D_INNER + SSM_CONV_DIM + SSM_N_HEADS

D_FF_DENSE = 2816
N_EXPERTS = 8
TOP_K = 2
D_FF_EXPERT = 3584

NORM_EPS = 1e-6
SSM_NORM_EPS = 1e-5
DT_MIN = 1e-3
DT_MAX = 1e-1

kernel_name = "hybrid_pool_ssd_moe_trunk"


def rmsnorm(x, w, eps=NORM_EPS):
    xf = x.astype(jnp.float32)
    y = xf * lax.rsqrt(jnp.mean(xf * xf, axis=-1, keepdims=True) + eps)
    return (y * w.astype(jnp.float32)).astype(x.dtype)


def swiglu(t, w_gate, w_up, w_down):
    return jnp.matmul(jax.nn.silu(jnp.matmul(t, w_gate)) * jnp.matmul(t, w_up), w_down)


def pool_mixer(h, w_grp, scale):
    b, l, d = h.shape
    hf = h.astype(jnp.float32).reshape(b, l, POOL_GROUPS, POOL_GROUP_DIM)
    cs0 = jnp.pad(jnp.cumsum(hf, axis=1), ((0, 0), (1, 0), (0, 0), (0, 0)))
    pos = jnp.arange(l)
    outs = []
    for g, w in enumerate(POOL_WINDOWS):
        upper = cs0[:, 1:, g]
        lower = jnp.pad(cs0[:, :l + 1 - w, g], ((0, 0), (w - 1, 0), (0, 0)))
        count = jnp.minimum(pos + 1, w).astype(jnp.float32)[None, :, None]
        outs.append((upper - lower) / count - hf[:, :, g])
    p = jnp.stack(outs, axis=2)
    y = jnp.einsum('blgc,gce->blge', p, w_grp.astype(jnp.float32)).reshape(b, l, d)
    return (y * scale.astype(jnp.float32)).astype(h.dtype)


def ssd_scan(xs, dt, a, bm, cm):
    b, l = xs.shape[:2]
    nc = l // SSM_CHUNK

    def to_chunks(t):
        return jnp.moveaxis(t.reshape(b, nc, SSM_CHUNK, *t.shape[2:]), 1, 0)

    causal = jnp.tril(jnp.ones((SSM_CHUNK, SSM_CHUNK), dtype=bool))[None, :, :, None, None]

    def step(state, inp):
        x_c, dt_c, b_c, c_c = inp
        acs = jnp.cumsum(dt_c * a, axis=1)
        seg = acs[:, :, None] - acs[:, None, :]
        decay = jnp.exp(jnp.where(causal, seg, -jnp.inf))
        cb = jnp.einsum('bign,bjgn->bijg', c_c, b_c)
        w = cb[..., None] * decay * dt_c[:, None]
        y_diag = jnp.einsum('bijgr,bjgrp->bigrp', w, x_c)
        y_off = jnp.einsum('bign,bgrpn->bigrp', c_c, state) * jnp.exp(acs)[..., None]
        to_end = jnp.exp(acs[:, -1:] - acs) * dt_c
        new_state = (state * jnp.exp(acs[:, -1])[..., None, None]
                     + jnp.einsum('bjgn,bjgr,bjgrp->bgrpn', b_c, to_end, x_c))
        return new_state, y_diag + y_off

    state0 = jnp.zeros((b, SSM_N_GROUPS, SSM_HEADS_PER_GROUP, SSM_HEAD_DIM, SSM_D_STATE), jnp.float32)
    _, y = lax.scan(step, state0, (to_chunks(xs), to_chunks(dt), to_chunks(bm), to_chunks(cm)))
    return jnp.moveaxis(y, 0, 1).reshape(xs.shape)


def ssd_mixer(h, w_in, conv_w, conv_b, dt_bias, a_log, d_skip, gate_norm_w, w_out):
    b, l, _ = h.shape
    f32 = jnp.float32
    zxbcdt = jnp.einsum('bld,de->ble', h, w_in).astype(f32)
    z = zxbcdt[..., :SSM_D_INNER]
    xbc = zxbcdt[..., SSM_D_INNER:SSM_D_INNER + SSM_CONV_DIM]
    dt = zxbcdt[..., SSM_D_INNER + SSM_CONV_DIM:]
    xbc = lax.conv_general_dilated(
        xbc, conv_w.astype(f32)[:, None, :], window_strides=(1,),
        padding=[(SSM_CONV - 1, 0)], dimension_numbers=('NWC', 'WIO', 'NWC'),
        feature_group_count=SSM_CONV_DIM) + conv_b.astype(f32)
    xbc = jax.nn.silu(xbc)
    xs = xbc[..., :SSM_D_INNER].reshape(b, l, SSM_N_GROUPS, SSM_HEADS_PER_GROUP, SSM_HEAD_DIM)
    bm = xbc[..., SSM_D_INNER:SSM_D_INNER + SSM_BC_DIM].reshape(b, l, SSM_N_GROUPS, SSM_D_STATE)
    cm = xbc[..., SSM_D_INNER + SSM_BC_DIM:].reshape(b, l, SSM_N_GROUPS, SSM_D_STATE)
    dt = jax.nn.softplus(dt + dt_bias.astype(f32)).reshape(b, l, SSM_N_GROUPS, SSM_HEADS_PER_GROUP)
    a = -jnp.exp(a_log.astype(f32)).reshape(SSM_N_GROUPS, SSM_HEADS_PER_GROUP)
    y = ssd_scan(xs, dt, a, bm, cm)
    y = y + d_skip.astype(f32).reshape(SSM_N_GROUPS, SSM_HEADS_PER_GROUP)[..., None] * xs
    y = y.reshape(b, l, SSM_D_INNER)
    y = rmsnorm(y * jax.nn.silu(z), gate_norm_w, SSM_NORM_EPS)
    return jnp.einsum('ble,ed->bld', y.astype(h.dtype), w_out).astype(h.dtype)


def moe_swiglu(h, w_router, w_gate, w_up, w_down):
    b, l, d = h.shape
    t = h.reshape(b * l, d)
    logits = jnp.matmul(t, w_router).astype(jnp.float32)
    top_val, top_idx = lax.top_k(logits, TOP_K)
    gates = jax.nn.softmax(top_val, axis=-1)
    combine = jnp.sum(jax.nn.one_hot(top_idx, N_EXPERTS, dtype=jnp.float32) * gates[..., None], axis=1)
    out = jnp.zeros((b * l, d), jnp.float32)
    for e in range(N_EXPERTS):
        y_e = swiglu(t, w_gate[e], w_up[e], w_down[e]).astype(jnp.float32)
        out = out + combine[:, e:e + 1] * y_e
    return out.reshape(b, l, d).astype(h.dtype)


def setup_inputs(seed: int = 0) -> dict:
    key = jax.random.key(seed)
    ks = jax.random.split(key, 32)
    f32 = jnp.float32

    def nrm(k, shape, fan_in):
        return jax.random.normal(k, shape, f32) * (fan_in ** -0.5)

    def gain(k, shape):
        return 1.0 + 0.1 * jax.random.normal(k, shape, f32)

    dt0 = jnp.exp(jax.random.uniform(ks[14], (N_ODD, SSM_N_HEADS), f32)
                  * (np.log(DT_MAX) - np.log(DT_MIN)) + np.log(DT_MIN))
    dt_bias = dt0 + jnp.log(-jnp.expm1(-dt0))
    return {
        "x": jax.random.normal(ks[0], (BATCH, SEQ, D_MODEL), f32),
        "pool_norm_w": gain(ks[1], (N_EVEN, D_MODEL)),
        "pool_w": nrm(ks[2], (N_EVEN, POOL_GROUPS, POOL_GROUP_DIM, POOL_GROUP_DIM), POOL_GROUP_DIM),
        "pool_scale": 0.5 + 0.05 * jax.random.normal(ks[3], (N_EVEN, D_MODEL), f32),
        "dense_norm_w": gain(ks[4], (N_EVEN, D_MODEL)),
        "dense_w_gate": nrm(ks[5], (N_EVEN, D_MODEL, D_FF_DENSE), D_MODEL),
        "dense_w_up": nrm(ks[6], (N_EVEN, D_MODEL, D_FF_DENSE), D_MODEL),
        "dense_w_down": nrm(ks[7], (N_EVEN, D_FF_DENSE, D_MODEL), D_FF_DENSE),
        "ssd_norm_w": gain(ks[8], (N_ODD, D_MODEL)),
        "ssd_w_in": nrm(ks[9], (N_ODD, D_MODEL, SSM_IN_DIM), D_MODEL),
        "ssd_conv_w": 0.5 * jax.random.normal(ks[10], (N_ODD, SSM_CONV, SSM_CONV_DIM), f32),
        "ssd_conv_b": 0.02 * jax.random.normal(ks[11], (N_ODD, SSM_CONV_DIM), f32),
        "ssd_dt_bias": dt_bias,
        "ssd_a_log": jnp.log(jax.random.uniform(ks[12], (N_ODD, SSM_N_HEADS), f32, 1.0, 16.0)),
        "ssd_d": gain(ks[13], (N_ODD, SSM_N_HEADS)),
        "ssd_gate_norm_w": gain(ks[15], (N_ODD, SSM_D_INNER)),
        "ssd_w_out": nrm(ks[16], (N_ODD, SSM_D_INNER, D_MODEL), SSM_D_INNER),
        "moe_norm_w": gain(ks[17], (N_ODD, D_MODEL)),
        "moe_w_router": nrm(ks[18], (N_ODD, D_MODEL, N_EXPERTS), D_MODEL),
        "moe_w_gate": nrm(ks[19], (N_ODD, N_EXPERTS, D_MODEL, D_FF_EXPERT), D_MODEL),
        "moe_w_up": nrm(ks[20], (N_ODD, N_EXPERTS, D_MODEL, D_FF_EXPERT), D_MODEL),
        "moe_w_down": nrm(ks[21], (N_ODD, N_EXPERTS, D_FF_EXPERT, D_MODEL), D_FF_EXPERT),
        "final_norm_w": gain(ks[22], (D_MODEL,)),
    }


def reference(x, pool_norm_w, pool_w, pool_scale, dense_norm_w, dense_w_gate, dense_w_up,
              dense_w_down, ssd_norm_w, ssd_w_in, ssd_conv_w, ssd_conv_b, ssd_dt_bias,
              ssd_a_log, ssd_d, ssd_gate_norm_w, ssd_w_out, moe_norm_w, moe_w_router,
              moe_w_gate, moe_w_up, moe_w_down, final_norm_w):
    for i in range(DEPTH):
        j = i // 2
        if i % 2 == 0:
            x = x + pool_mixer(rmsnorm(x, pool_norm_w[j]), pool_w[j], pool_scale[j])
            x = x + swiglu(rmsnorm(x, dense_norm_w[j]), dense_w_gate[j], dense_w_up[j],
                           dense_w_down[j]).astype(x.dtype)
        else:
            x = x + ssd_mixer(rmsnorm(x, ssd_norm_w[j]), ssd_w_in[j], ssd_conv_w[j], ssd_conv_b[j],
                              ssd_dt_bias[j], ssd_a_log[j], ssd_d[j], ssd_gate_norm_w[j], ssd_w_out[j])
            x = x + moe_swiglu(rmsnorm(x, moe_norm_w[j]), moe_w_router[j], moe_w_gate[j],
                               moe_w_up[j], moe_w_down[j])
    return rmsnorm(x, final_norm_w)
```

```python
import functools

import jax
import jax.numpy as jnp
from jax import lax
from jax.experimental import pallas as pl
from jax.experimental.pallas import tpu as pltpu

F32 = jnp.float32
BF16 = jnp.bfloat16
I32 = jnp.int32

D_MODEL = 1024
POOL_WINDOWS = (2, 4, 8, 16)
POOL_GROUP_DIM = D_MODEL // len(POOL_WINDOWS)
POOL_HALO = 16

SSM_D_INNER = 2048
SSM_HEAD_DIM = 64
SSM_N_HEADS = 32
SSM_N_GROUPS = 8
SSM_HEADS_PER_GROUP = 4
SSM_D_STATE = 128
SSM_CONV = 4
SSM_CHUNK = 128
SSM_BC_DIM = SSM_N_GROUPS * SSM_D_STATE
SSM_CONV_DIM = SSM_D_INNER + 2 * SSM_BC_DIM
SSM_GROUP_WIDTH = SSM_HEADS_PER_GROUP * SSM_HEAD_DIM
CONV_CARRY = 8

D_FF_DENSE = 2816
N_EXPERTS = 8
D_FF_EXPERT = 3584

NORM_EPS = 1e-6
SSM_NORM_EPS = 1e-5

LANES = 128
V7X_VMEM_LIMIT_BYTES = 56 * 1024 * 1024

TOKEN_TILE = 512
MOE_ROW_TILE = 512
MOE_FF_TILE = 1792
DENSE_FF_SPLITS = ((0, 1536), (1536, 2816))
SMEM_BLOCK = 1024
DMA_WAIT_UNROLL = 16


def _params(semantics):
    return pltpu.CompilerParams(dimension_semantics=semantics,
                                vmem_limit_bytes=V7X_VMEM_LIMIT_BYTES)


def _const_spec(shape):
    nd = len(shape)
    return pl.BlockSpec(shape, lambda *_: (0,) * nd, pipeline_mode=pl.Buffered(1))


def _rms(x, w, eps):
    return x * lax.rsqrt(jnp.mean(x * x, axis=-1, keepdims=True) + eps) * w


def _silu(v):
    return v * jax.nn.sigmoid(v)


def _dot(a, b):
    return jnp.dot(a, b, preferred_element_type=F32)


def _dot_exact(a, b):
    return jnp.dot(a, b, preferred_element_type=F32, precision=lax.Precision.HIGHEST)


def _pool_kernel(x_ref, halo_ref, nw_ref, pw_ref, ps_ref, o_ref):
    i = pl.program_id(1)
    tl = x_ref.shape[0]
    x = x_ref[...]
    nw = nw_ref[...]
    hn = _rms(x, nw, NORM_EPS)
    hh = jnp.where(i > 0, _rms(halo_ref[...], nw, NORM_EPS), 0.0)
    cat = jnp.concatenate([hh, hn], axis=0)
    pos = i * tl + lax.broadcasted_iota(I32, (tl, 1), 0)
    outs = []
    for g, w in enumerate(POOL_WINDOWS):
        lo = g * POOL_GROUP_DIM
        s = cat[:, lo:lo + POOL_GROUP_DIM]
        k = 1
        while k < w:
            s = s + pltpu.roll(s, k, 0)
            k *= 2
        count = jnp.minimum(pos + 1, w).astype(F32)
        p = s[POOL_HALO:] / count - hn[:, lo:lo + POOL_GROUP_DIM]
        outs.append(_dot(p.astype(BF16), pw_ref[g]))
    y = jnp.concatenate(outs, axis=1) * ps_ref[...]
    o_ref[...] = x + y


def _pool_layer(x, norm_w, pool_w, pool_scale):
    b, l, d = x.shape
    tl = TOKEN_TILE
    halo_blocks = tl // POOL_HALO
    return pl.pallas_call(
        _pool_kernel,
        grid=(b, l // tl),
        in_specs=[
            pl.BlockSpec((None, tl, d), lambda bi, i: (bi, i, 0)),
            pl.BlockSpec((None, POOL_HALO, d),
                         lambda bi, i: (bi, jnp.maximum(i * halo_blocks - 1, 0), 0)),
            _const_spec((1, d)),
            _const_spec(pool_w.shape),
            _const_spec((1, d)),
        ],
        out_specs=pl.BlockSpec((None, tl, d), lambda bi, i: (bi, i, 0)),
        out_shape=jax.ShapeDtypeStruct(x.shape, F32),
        compiler_params=_params(("arbitrary", "arbitrary")),
        name="pool_mixer",
    )(x, x, norm_w.reshape(1, d), pool_w.astype(BF16), pool_scale.reshape(1, d))


def _dense_kernel(x_ref, nw_ref, wg_ref, wu_ref, wd_ref, o_ref):
    x = x_ref[...]
    xn = _rms(x, nw_ref[...], NORM_EPS).astype(BF16)
    acc = x
    for lo, hi in DENSE_FF_SPLITS:
        g = _dot(xn, wg_ref[:, lo:hi])
        u = _dot(xn, wu_ref[:, lo:hi])
        h = (_silu(g) * u).astype(BF16)
        acc = acc + _dot(h, wd_ref[lo:hi, :])
    o_ref[...] = acc


def _dense_layer(x, norm_w, w_gate, w_up, w_down):
    t, d = x.shape
    tm = TOKEN_TILE
    return pl.pallas_call(
        _dense_kernel,
        grid=(t // tm,),
        in_specs=[
            pl.BlockSpec((tm, d), lambda i: (i, 0)),
            _const_spec((1, d)),
            _const_spec(w_gate.shape),
            _const_spec(w_up.shape),
            _const_spec(w_down.shape),
        ],
        out_specs=pl.BlockSpec((tm, d), lambda i: (i, 0)),
        out_shape=jax.ShapeDtypeStruct((t, d), F32),
        compiler_params=_params(("arbitrary",)),
        name="dense_swiglu",
    )(x, norm_w.reshape(1, d), w_gate.astype(BF16), w_up.astype(BF16), w_down.astype(BF16))


def _ssd_in_kernel(x_ref, nw_ref, wz_ref, wx_ref, wdt_ref, cw_ref, cb_ref, dtb_ref,
                   z_ref, xs_ref, bm_ref, cm_ref, dt_ref, carry_ref):
    i = pl.program_id(1)
    tl = x_ref.shape[0]

    @pl.when(i == 0)
    def _():
        carry_ref[...] = jnp.zeros_like(carry_ref)

    xn = _rms(x_ref[...], nw_ref[...], NORM_EPS).astype(BF16)
    z_ref[...] = _dot(xn, wz_ref[...]).astype(BF16)
    v = _dot(xn, wdt_ref[...]) + dtb_ref[...]
    dt_ref[...] = jnp.maximum(v, 0.0) + jnp.log1p(jnp.exp(-jnp.abs(v)))

    width = SSM_BC_DIM
    for c in range(SSM_CONV_DIM // width):
        lo = c * width
        xbc = _dot(xn, wx_ref[:, lo:lo + width])
        cat = jnp.concatenate([carry_ref[:, lo:lo + width], xbc], axis=0)
        carry_ref[:, lo:lo + width] = xbc[tl - CONV_CARRY:, :]
        cw = cw_ref[:, lo:lo + width]
        acc = cat * cw[SSM_CONV - 1:SSM_CONV]
        for k in range(1, SSM_CONV):
            acc = acc + pltpu.roll(cat, k, 0) * cw[SSM_CONV - 1 - k:SSM_CONV - k]
        act = _silu(acc[CONV_CARRY:] + cb_ref[:, lo:lo + width]).astype(BF16)
        if lo < SSM_D_INNER:
            xs_ref[:, lo:lo + width] = act
        elif lo < SSM_D_INNER + SSM_BC_DIM:
            bm_ref[...] = act
        else:
            cm_ref[...] = act


def _ssd_in_layer(x, norm_w, w_in, conv_w, conv_b, dt_bias):
    b, l, d = x.shape
    tl = TOKEN_TILE
    w_z = w_in[:, :SSM_D_INNER].astype(BF16)
    w_x = w_in[:, SSM_D_INNER:SSM_D_INNER + SSM_CONV_DIM].astype(BF16)
    w_dt = jnp.pad(w_in[:, SSM_D_INNER + SSM_CONV_DIM:], ((0, 0), (0, LANES - SSM_N_HEADS))).astype(BF16)
    dtb = jnp.pad(dt_bias, (0, LANES - SSM_N_HEADS)).reshape(1, LANES)

    def seq(width):
        return pl.BlockSpec((None, tl, width), lambda bi, i: (bi, i, 0))

    return pl.pallas_call(
        _ssd_in_kernel,
        grid=(b, l // tl),
        in_specs=[
            seq(d),
            _const_spec((1, d)),
            _const_spec(w_z.shape),
            _const_spec(w_x.shape),
            _const_spec(w_dt.shape),
            _const_spec(conv_w.shape),
            _const_spec((1, SSM_CONV_DIM)),
            _const_spec((1, LANES)),
        ],
        out_specs=[seq(SSM_D_INNER), seq(SSM_D_INNER), seq(SSM_BC_DIM), seq(SSM_BC_DIM), seq(LANES)],
        out_shape=[
            jax.ShapeDtypeStruct((b, l, SSM_D_INNER), BF16),
            jax.ShapeDtypeStruct((b, l, SSM_D_INNER), BF16),
            jax.ShapeDtypeStruct((b, l, SSM_BC_DIM), BF16),
            jax.ShapeDtypeStruct((b, l, SSM_BC_DIM), BF16),
            jax.ShapeDtypeStruct((b, l, LANES), F32),
        ],
        scratch_shapes=[pltpu.VMEM((CONV_CARRY, SSM_CONV_DIM), F32)],
        compiler_params=_params(("arbitrary", "arbitrary")),
        name="ssd_in_proj_conv",
    )(x, norm_w.reshape(1, d), w_z, w_x, w_dt, conv_w, conv_b.reshape(1, SSM_CONV_DIM), dtb)


def _ssd_scan_kernel(xs_ref, bm_ref, cm_ref, z_ref, dt_ref, x_ref,
                     alog_ref, dskip_ref, gnw_ref, ex_ref, wout_ref,
                     o_ref, state_ref, yn_ref):
    i = pl.program_id(1)
    q = SSM_CHUNK
    n_chunks = xs_ref.shape[0] // q

    @pl.when(i == 0)
    def _():
        state_ref[...] = jnp.zeros_like(state_ref)

    a_row = -jnp.exp(alog_ref[...])
    rows = lax.broadcasted_iota(I32, (q, q), 0)
    cols = lax.broadcasted_iota(I32, (q, q), 1)
    causal = rows >= cols
    tril = causal.astype(F32)
    ex = ex_ref[...]
    ex_b = ex.astype(BF16)
    first_head = lax.broadcasted_iota(I32, (q, LANES), 1) < SSM_HEAD_DIM

    def chunk(c, carry):
        sl = pl.ds(pl.multiple_of(c * q, q), q)
        dt = dt_ref[sl, :]
        acs = _dot_exact(tril, dt * a_row)
        acs_t = acs.T
        dt_t = dt.T
        acs_end = acs[q - 1:q, :]
        grow = jnp.exp(acs)
        to_end_dt = jnp.exp(acs_end - acs) * dt
        to_end_x = _dot(to_end_dt.astype(BF16), ex_b)
        end_x = _dot_exact(jnp.broadcast_to(jnp.exp(acs_end), (8, LANES)), ex)[0:1]
        xs_b = xs_ref[sl, :]
        xs = xs_b.astype(F32)
        xw_b = (xs * to_end_x).astype(BF16)
        ys = []
        for g in range(SSM_N_GROUPS):
            clo = g * SSM_GROUP_WIDTH
            bm_g = bm_ref[sl, g * SSM_D_STATE:(g + 1) * SSM_D_STATE]
            cm_g = cm_ref[sl, g * SSM_D_STATE:(g + 1) * SSM_D_STATE]
            cb = lax.dot_general(cm_g, bm_g, (((1,), (1,)), ((), ())), preferred_element_type=F32)
            cm_f = cm_g.astype(F32)
            st = state_ref[g]
            st_b = st.astype(BF16)
            for pair in range(SSM_HEADS_PER_GROUP // 2):
                plo = pair * LANES
                rhs = jnp.concatenate([xs_b[:, clo + plo:clo + plo + LANES], st_b[:, plo:plo + LANES]], axis=0)
                lhs = []
                for r in range(2):
                    h = g * SSM_HEADS_PER_GROUP + pair * 2 + r
                    seg = acs[:, h:h + 1] - acs_t[h:h + 1, :]
                    decay = jnp.exp(jnp.where(causal, seg, -jnp.inf))
                    wmat = cb * decay * dt_t[h:h + 1, :]
                    lhs.append(jnp.concatenate([wmat, cm_f * grow[:, h:h + 1]], axis=1).astype(BF16))
                both = _dot(jnp.concatenate(lhs, axis=0), rhs)
                ys.append(jnp.where(first_head, both[:q], both[q:]))
            upd = lax.dot_general(bm_g, xw_b[:, clo:clo + SSM_GROUP_WIDTH], (((0,), (0,)), ((), ())),
                                  preferred_element_type=F32)
            state_ref[g] = st * end_x[:, clo:clo + SSM_GROUP_WIDTH] + upd
        y = jnp.concatenate(ys, axis=1) + dskip_ref[...] * xs
        y = y * _silu(z_ref[sl, :].astype(F32))
        yn_ref[sl, :] = _rms(y, gnw_ref[...], SSM_NORM_EPS).astype(BF16)
        return carry

    lax.fori_loop(0, n_chunks, chunk, 0)
    o_ref[...] = x_ref[...] + _dot(yn_ref[...], wout_ref[...])


def _ssd_scan_layer(x, z, xs, bm, cm, dt, a_log, d_skip, gate_norm_w, w_out):
    b, l, d = x.shape
    tl = TOKEN_TILE
    alog = jnp.pad(a_log, (0, LANES - SSM_N_HEADS)).reshape(1, LANES)
    dskip = jnp.repeat(d_skip, SSM_HEAD_DIM).reshape(1, SSM_D_INNER)
    head_of_channel = jnp.arange(SSM_D_INNER, dtype=I32) // SSM_HEAD_DIM
    expand = (jnp.arange(LANES, dtype=I32)[:, None] == head_of_channel[None, :]).astype(F32)

    def seq(width):
        return pl.BlockSpec((None, tl, width), lambda bi, i: (bi, i, 0))

    return pl.pallas_call(
        _ssd_scan_kernel,
        grid=(b, l // tl),
        in_specs=[
            seq(SSM_D_INNER), seq(SSM_BC_DIM), seq(SSM_BC_DIM), seq(SSM_D_INNER), seq(LANES), seq(d),
            _const_spec((1, LANES)),
            _const_spec((1, SSM_D_INNER)),
            _const_spec((1, SSM_D_INNER)),
            _const_spec(expand.shape),
            _const_spec(w_out.shape),
        ],
        out_specs=seq(d),
        out_shape=jax.ShapeDtypeStruct((b, l, d), F32),
        scratch_shapes=[
            pltpu.VMEM((SSM_N_GROUPS, SSM_D_STATE, SSM_GROUP_WIDTH), F32),
            pltpu.VMEM((tl, SSM_D_INNER), BF16),
        ],
        compiler_params=_params(("arbitrary", "arbitrary")),
        name="ssd_scan_out_proj",
    )(xs, bm, cm, z, dt, x, alog, dskip, gate_norm_w.reshape(1, SSM_D_INNER), expand,
      w_out.astype(BF16))


def _router_kernel(x_ref, nw_ref, whi_ref, wlo_ref, rt_ref, rc_ref):
    xn = _rms(x_ref[...], nw_ref[...], NORM_EPS)
    hi = xn.astype(BF16)
    lo = (xn - hi.astype(F32)).astype(BF16)
    logits = _dot(hi, whi_ref[...]) + _dot(lo, whi_ref[...]) + _dot(hi, wlo_ref[...])
    col = lax.broadcasted_iota(I32, logits.shape, 1).astype(F32)
    lg = jnp.where(col < N_EXPERTS, logits, -jnp.inf)
    m1 = jnp.max(lg, axis=1, keepdims=True)
    i1 = jnp.min(jnp.where(lg == m1, col, float(LANES)), axis=1, keepdims=True)
    lg2 = jnp.where(col == i1, -jnp.inf, lg)
    m2 = jnp.max(lg2, axis=1, keepdims=True)
    i2 = jnp.min(jnp.where(lg2 == m2, col, float(LANES)), axis=1, keepdims=True)
    e2 = jnp.exp(m2 - m1)
    g1 = 1.0 / (1.0 + e2)
    g2 = e2 / (1.0 + e2)
    packed = jnp.where(col == 0, i1,
                       jnp.where(col == 1, i2, jnp.where(col == 2, g1, jnp.where(col == 3, g2, 0.0))))
    rc_ref[...] = packed[:, :N_EXPERTS]
    rt_ref[...] = packed.T[:N_EXPERTS, :]


def _router(x, norm_w, w_router):
    t, d = x.shape
    tm = TOKEN_TILE
    w_pad = jnp.pad(w_router, ((0, 0), (0, LANES - N_EXPERTS)))
    w_hi = w_pad.astype(BF16)
    w_lo = (w_pad - w_hi.astype(F32)).astype(BF16)
    return pl.pallas_call(
        _router_kernel,
        grid=(t // tm,),
        in_specs=[
            pl.BlockSpec((tm, d), lambda i: (i, 0)),
            _const_spec((1, d)),
            _const_spec(w_hi.shape),
            _const_spec(w_lo.shape),
        ],
        out_specs=[
            pl.BlockSpec((N_EXPERTS, tm), lambda i: (0, i)),
            pl.BlockSpec((tm, N_EXPERTS), lambda i: (i, 0)),
        ],
        out_shape=[
            jax.ShapeDtypeStruct((N_EXPERTS, t), F32),
            jax.ShapeDtypeStruct((t, N_EXPERTS), F32),
        ],
        compiler_params=_params(("arbitrary",)),
        name="moe_router",
    )(x, norm_w.reshape(1, d), w_hi, w_lo)


def _plan_kernel(i1_ref, i2_ref, pos_ref, tile_ref, pad_ref):
    nb = i1_ref.shape[0]
    tm = MOE_ROW_TILE
    i1 = i1_ref[...]
    i2 = i2_ref[...]
    r = lax.broadcasted_iota(I32, (LANES, LANES), 0)
    c = lax.broadcasted_iota(I32, (LANES, LANES), 1)
    upper = (r <= c).astype(BF16)
    ones_l = jnp.ones((LANES, LANES), BF16)
    rb = lax.broadcasted_iota(I32, (nb, nb), 0)
    cb = lax.broadcasted_iota(I32, (nb, nb), 1)
    below = (cb < rb).astype(BF16)
    ones_b = jnp.ones((nb, nb), BF16)
    tile_start = (lax.broadcasted_iota(I32, tile_ref.shape, 0) * LANES
                  + lax.broadcasted_iota(I32, tile_ref.shape, 1)) * tm
    start = jnp.zeros((nb, LANES), I32)
    pos1 = jnp.zeros((nb, LANES), I32)
    pos2 = jnp.zeros((nb, LANES), I32)
    tile_expert = jnp.zeros(tile_ref.shape, I32)
    lane = lax.broadcasted_iota(I32, pad_ref.shape, 1)
    pads = jnp.zeros(pad_ref.shape, I32)
    for e in range(N_EXPERTS):
        oh1 = i1 == e
        oh2 = i2 == e
        oh = jnp.where(oh1 | oh2, 1.0, 0.0).astype(BF16)
        within = _dot(oh, upper)
        row_total = _dot(oh, ones_l).astype(BF16)
        before = _dot(below, row_total)
        total = _dot(ones_b, row_total).astype(I32)
        slot = start + (before + within).astype(I32) - 1
        pos1 = jnp.where(oh1, slot, pos1)
        pos2 = jnp.where(oh2, slot, pos2)
        pad_lo = start + total
        start = start + ((total + (tm - 1)) // tm) * tm
        tile_expert = tile_expert + (tile_start >= start[:tile_ref.shape[0], :]).astype(I32)
        pads = jnp.where(lane == e, pad_lo[:pad_ref.shape[0], :],
                         jnp.where(lane == N_EXPERTS + e, start[:pad_ref.shape[0], :], pads))
    pos_ref[0] = pos1
    pos_ref[1] = pos2
    tile_ref[...] = tile_expert
    pad_ref[...] = pads


def _route_plan(i1, i2):
    nb = i1.shape[0]
    return pl.pallas_call(
        _plan_kernel,
        out_shape=[
            jax.ShapeDtypeStruct((2, nb, LANES), I32),
            jax.ShapeDtypeStruct((8, LANES), I32),
            jax.ShapeDtypeStruct((8, LANES), I32),
        ],
        compiler_params=pltpu.CompilerParams(vmem_limit_bytes=V7X_VMEM_LIMIT_BYTES),
        name="moe_route_plan",
    )(i1, i2)


def _row_copy(src, src_row, dst, dst_row, sem):
    return pltpu.make_async_copy(src.at[pl.ds(src_row, 1)], dst.at[pl.ds(dst_row, 1)], sem)


def _wait_rows(src, dst, sem, n_rows):
    def body(_, carry):
        for _ in range(DMA_WAIT_UNROLL):
            _row_copy(src, 0, dst, 0, sem).wait()
        return carry
    lax.fori_loop(0, n_rows // DMA_WAIT_UNROLL, body, 0)


def _dispatch_kernel(pad_ref, pos_ref, x_ref, xs_ref, zero_ref, sem, zsem):
    i = pl.program_id(0)
    tb = x_ref.shape[0]

    @pl.when(i == 0)
    def _():
        zero_ref[...] = jnp.zeros_like(zero_ref)
        for e in range(N_EXPERTS):
            lo = pad_ref[e]
            hi = pad_ref[N_EXPERTS + e]

            def fill(row, carry):
                _row_copy(zero_ref, 0, xs_ref, row, zsem).start()
                return carry

            def drain(row, carry):
                _row_copy(zero_ref, 0, xs_ref, row, zsem).wait()
                return carry

            lax.fori_loop(lo, hi, fill, 0)
            lax.fori_loop(lo, hi, drain, 0)

        first_blk = pad_ref[2 * N_EXPERTS - 1] // 8
        last_blk = xs_ref.shape[0] // 8

        def tail_copy(blk):
            dst = xs_ref.at[pl.ds(pl.multiple_of(blk * 8, 8), 8)]
            return pltpu.make_async_copy(zero_ref, dst, zsem)

        def tail_fill(blk, carry):
            tail_copy(blk).start()
            return carry

        def tail_drain(blk, carry):
            tail_copy(blk).wait()
            return carry

        lax.fori_loop(first_blk, last_blk, tail_fill, 0)
        lax.fori_loop(first_blk, last_blk, tail_drain, 0)

    def issue(t, carry):
        _row_copy(x_ref, t, xs_ref, pos_ref[2 * t], sem).start()
        _row_copy(x_ref, t, xs_ref, pos_ref[2 * t + 1], sem).start()
        return carry

    lax.fori_loop(0, tb, issue, 0, unroll=8)
    _wait_rows(x_ref, xs_ref, sem, 2 * tb)


def _dispatch(x, pos_flat, pads, n_rows):
    t, d = x.shape
    tb = TOKEN_TILE
    assert 2 * tb == SMEM_BLOCK
    return pl.pallas_call(
        _dispatch_kernel,
        grid_spec=pltpu.PrefetchScalarGridSpec(
            num_scalar_prefetch=1,
            grid=(t // tb,),
            in_specs=[
                pl.BlockSpec((2 * tb,), lambda i, pad: (i,), memory_space=pltpu.SMEM),
                pl.BlockSpec((tb, d), lambda i, pad: (i, 0)),
            ],
            out_specs=pl.BlockSpec(memory_space=pl.ANY),
            scratch_shapes=[
                pltpu.VMEM((8, d), F32),
                pltpu.SemaphoreType.DMA,
                pltpu.SemaphoreType.DMA,
            ],
        ),
        out_shape=jax.ShapeDtypeStruct((n_rows, d), F32),
        compiler_params=_params(("arbitrary",)),
        name="moe_dispatch",
    )(pads, pos_flat, x)


def _expert_kernel(te_ref, na_ref, x_ref, nw_ref, wg_ref, wu_ref, wd_ref, y_ref, xn_ref, acc_ref):
    i = pl.program_id(0)
    f = pl.program_id(1)
    nf = pl.num_programs(1)
    active = i < na_ref[0]

    @pl.when(active & (f == 0))
    def _():
        xn_ref[...] = _rms(x_ref[...], nw_ref[...], NORM_EPS).astype(BF16)
        acc_ref[...] = jnp.zeros_like(acc_ref)

    @pl.when(active)
    def _():
        xn = xn_ref[...]
        h = (_silu(_dot(xn, wg_ref[...])) * _dot(xn, wu_ref[...])).astype(BF16)
        acc_ref[...] += _dot(h, wd_ref[...])

    @pl.when(active & (f == nf - 1))
    def _():
        y_ref[...] = acc_ref[...]

    @pl.when(jnp.logical_not(active) & (f == 0))
    def _():
        y_ref[...] = jnp.zeros_like(y_ref)


def _experts(xs, norm_w, tile_expert, n_active, w_gate, w_up, w_down):
    n_rows, d = xs.shape
    tm, fc = MOE_ROW_TILE, MOE_FF_TILE
    nf = D_FF_EXPERT // fc

    def row_map(i, f, te, na):
        return (jnp.minimum(i, na[0] - 1), 0)

    def out_map(i, f, te, na):
        return (i, 0)

    def ff(i, f, na):
        return jnp.where(i < na[0], f, nf - 1)

    def gate_map(i, f, te, na):
        return (te[jnp.minimum(i, na[0] - 1)], 0, ff(i, f, na))

    def down_map(i, f, te, na):
        return (te[jnp.minimum(i, na[0] - 1)], ff(i, f, na), 0)

    return pl.pallas_call(
        _expert_kernel,
        grid_spec=pltpu.PrefetchScalarGridSpec(
            num_scalar_prefetch=2,
            grid=(n_rows // tm, nf),
            in_specs=[
                pl.BlockSpec((tm, d), row_map),
                pl.BlockSpec((1, d), lambda i, f, te, na: (0, 0)),
                pl.BlockSpec((None, d, fc), gate_map),
                pl.BlockSpec((None, d, fc), gate_map),
                pl.BlockSpec((None, fc, d), down_map),
            ],
            out_specs=pl.BlockSpec((tm, d), out_map),
            scratch_shapes=[pltpu.VMEM((tm, d), BF16), pltpu.VMEM((tm, d), F32)],
        ),
        out_shape=jax.ShapeDtypeStruct((n_rows, d), F32),
        compiler_params=_params(("arbitrary", "arbitrary")),
        name="moe_experts",
    )(tile_expert, n_active, xs, norm_w.reshape(1, d), w_gate, w_up, w_down)


def _combine_kernel(pos_ref, x_ref, rc_ref, fw_ref, y_ref, o_ref, ybuf_ref, sem):
    tb = x_ref.shape[0]

    def issue(t, carry):
        _row_copy(y_ref, pos_ref[2 * t], ybuf_ref.at[0], t, sem).start()
        _row_copy(y_ref, pos_ref[2 * t + 1], ybuf_ref.at[1], t, sem).start()
        return carry

    lax.fori_loop(0, tb, issue, 0, unroll=8)
    _wait_rows(y_ref, ybuf_ref.at[0], sem, 2 * tb)
    rc = rc_ref[...]
    out = x_ref[...] + rc[:, 2:3] * ybuf_ref[0] + rc[:, 3:4] * ybuf_ref[1]
    o_ref[...] = _rms(out, fw_ref[...], NORM_EPS)


def _combine(x, rc, pos_flat, y, final_w):
    t, d = x.shape
    tb = TOKEN_TILE
    assert 2 * tb == SMEM_BLOCK
    return pl.pallas_call(
        _combine_kernel,
        grid=(t // tb,),
        in_specs=[
            pl.BlockSpec((2 * tb,), lambda i: (i,), memory_space=pltpu.SMEM),
            pl.BlockSpec((tb, d), lambda i: (i, 0)),
            pl.BlockSpec((tb, N_EXPERTS), lambda i: (i, 0)),
            _const_spec((1, d)),
            pl.BlockSpec(memory_space=pl.ANY),
        ],
        out_specs=pl.BlockSpec((tb, d), lambda i: (i, 0)),
        out_shape=jax.ShapeDtypeStruct((t, d), F32),
        scratch_shapes=[pltpu.VMEM((2, tb, d), F32), pltpu.SemaphoreType.DMA],
        compiler_params=_params(("arbitrary",)),
        name="moe_combine_final_norm",
    )(pos_flat, x, rc, final_w.reshape(1, d), y)


def _moe_and_final_norm(x, norm_w, w_router, w_gate, w_up, w_down, final_w):
    t, d = x.shape
    tm = MOE_ROW_TILE
    nb = t // LANES
    n_tiles = 2 * t // tm + N_EXPERTS
    assert n_tiles <= 8 * LANES
    rt, rc = _router(x, norm_w, w_router)
    i1 = rt[0].astype(I32).reshape(nb, LANES)
    i2 = rt[1].astype(I32).reshape(nb, LANES)
    pos, tiles, pads = _route_plan(i1, i2)
    pos_flat = pos.reshape(2, t).T.reshape(2 * t)
    pad_bounds = pads[0, :2 * N_EXPERTS]
    n_active = pad_bounds[2 * N_EXPERTS - 1:] // tm
    tile_expert = tiles.reshape(-1)[:n_tiles]
    xs = _dispatch(x, pos_flat, pad_bounds, n_tiles * tm)
    y = _experts(xs, norm_w, tile_expert, n_active,
                 w_gate.astype(BF16), w_up.astype(BF16), w_down.astype(BF16))
    return _combine(x, rc, pos_flat, y, final_w)


def kernel(x, pool_norm_w, pool_w, pool_scale, dense_norm_w, dense_w_gate, dense_w_up, dense_w_down,
           ssd_norm_w, ssd_w_in, ssd_conv_w, ssd_conv_b, ssd_dt_bias, ssd_a_log, ssd_d,
           ssd_gate_norm_w, ssd_w_out, moe_norm_w, moe_w_router, moe_w_gate, moe_w_up, moe_w_down,
           final_norm_w):
    b, l, d = x.shape
    t = b * l
    x = _pool_layer(x, pool_norm_w[0], pool_w[0], pool_scale[0])
    x = _dense_layer(x.reshape(t, d), dense_norm_w[0], dense_w_gate[0], dense_w_up[0], dense_w_down[0])
    x = x.reshape(b, l, d)
    z, xs, bm, cm, dt = _ssd_in_layer(x, ssd_norm_w[0], ssd_w_in[0], ssd_conv_w[0], ssd_conv_b[0],
                                      ssd_dt_bias[0])
    x = _ssd_scan_layer(x, z, xs, bm, cm, dt, ssd_a_log[0], ssd_d[0], ssd_gate_norm_w[0], ssd_w_out[0])
    out = _moe_and_final_norm(x.reshape(t, d), moe_norm_w[0], moe_w_router[0], moe_w_gate[0],
                              moe_w_up[0], moe_w_down[0], final_norm_w)
    return out.reshape(b, l, d)
```
